```python
import jax, jax.numpy as jnp
from jax import lax
import numpy as np

D_MODEL = 1024
BATCH = 8
SEQ = 4096
DEPTH = 1

CHUNK = 64
N_META = 16
HG_DK = 128
HG_DV = 128
HG_HEADS = D_MODEL // HG_DK
HG_WIDTH = HG_HEADS * HG_DK
SUB = 16
POOL_WINDOWS = (2, 4, 8, 16)
POOL_GROUPS = 4
POOL_WIDTH = D_MODEL // 2
POOL_GC = POOL_WIDTH // POOL_GROUPS
D_FF = -(-8 * D_MODEL // (3 * 256)) * 256
EPS = 1e-6
IN_COLS = 4 * HG_WIDTH + POOL_WIDTH + 2 * D_MODEL
SPLIT_IDX = (HG_WIDTH, 2 * HG_WIDTH, 3 * HG_WIDTH, 4 * HG_WIDTH,
             4 * HG_WIDTH + POOL_WIDTH, 4 * HG_WIDTH + POOL_WIDTH + D_MODEL)

kernel_name = "hgrn2_pool_gated_hybrid_block"


def rmsnorm(x, g):
    xf = x.astype(jnp.float32)
    y = xf * lax.rsqrt(jnp.mean(xf * xf, axis=-1, keepdims=True) + EPS)
    return (y * g.astype(jnp.float32)).astype(x.dtype)


def _hgrn2_chunk(state, inp):
    q, k, v, lg = inp
    bsz, nh, c, dk = q.shape
    n = c // SUB
    b = jnp.cumsum(lg, axis=2)
    b_last = b[:, :, -1]
    o_inter = jnp.einsum('bhck,bhkv->bhcv', q * jnp.exp(b), state)
    qr = q.reshape(bsz, nh, n, SUB, dk)
    kr = k.reshape(bsz, nh, n, SUB, dk)
    vr = v.reshape(bsz, nh, n, SUB, -1)
    br = b.reshape(bsz, nh, n, SUB, dk)
    tri = jnp.tril(jnp.ones((SUB, SUB), dtype=bool))
    diff = br[:, :, :, :, None, :] - br[:, :, :, None, :, :]
    decay = jnp.exp(jnp.where(tri[:, :, None], diff, -jnp.inf))
    a_diag = jnp.einsum('bhntd,bhnsd,bhntsd->bhnts', qr, kr, decay)
    e = br[:, :, :, -1]
    q_off = qr[:, :, :, :, None, :] * jnp.exp(
        jnp.minimum(br[:, :, :, :, None, :] - e[:, :, None, None, :, :], 0.0))
    k_off = kr * jnp.exp(e[:, :, :, None, :] - br)
    a_off = jnp.einsum('bhitjd,bhjsd->bhitjs', q_off, k_off)
    lower = jnp.tril(jnp.ones((n, n), jnp.float32), -1)[None, None, :, None, :, None]
    eye = jnp.eye(n, dtype=jnp.float32)[None, None, :, None, :, None]
    a = a_off * lower + a_diag[:, :, :, :, None, :] * eye
    o_intra = jnp.einsum('bhitjs,bhjsv->bhitv', a, vr).reshape(bsz, nh, c, -1)
    new_state = jnp.exp(b_last)[..., None] * state + jnp.einsum(
        'bhck,bhcv->bhkv', k * jnp.exp(b_last[:, :, None] - b), v)
    return new_state, o_inter + o_intra


def hgrn2_mixer(q_pre, f_pre, i_pre, lb):
    f32 = jnp.float32
    bsz, t_len, _ = q_pre.shape
    lbf = lb.astype(f32)
    fp = f_pre.astype(f32)
    q = jax.nn.silu(q_pre.astype(f32))
    lg = jnp.logaddexp(jnp.log(lbf), jnp.log1p(-lbf) + jax.nn.log_sigmoid(fp))
    k = (1.0 - lbf) * jax.nn.sigmoid(-fp)
    v = i_pre.astype(f32)
    pad = (-t_len) % CHUNK
    padt = lambda a: jnp.pad(a, ((0, 0), (pad, 0), (0, 0)))
    q, k, v, lg = padt(q), padt(k), padt(v), padt(lg)
    n_chunks = (t_len + pad) // CHUNK
    to_chunks = lambda a, dh: a.reshape(bsz, n_chunks, CHUNK, HG_HEADS, dh).transpose(1, 0, 3, 2, 4)
    xs = (to_chunks(q, HG_DK), to_chunks(k, HG_DK), to_chunks(v, HG_DV), to_chunks(lg, HG_DK))
    state0 = jnp.zeros((bsz, HG_HEADS, HG_DK, HG_DV), f32)
    _, o = lax.scan(_hgrn2_chunk, state0, xs)
    o = o.transpose(1, 0, 3, 2, 4).reshape(bsz, n_chunks * CHUNK, HG_HEADS * HG_DV)
    return o[:, pad:]


def pool_mixer(xp, w_grp, scale):
    f32 = jnp.float32
    bsz, t_len, _ = xp.shape
    xf = xp.astype(f32)
    cs = jnp.pad(jnp.cumsum(xf, axis=1), ((0, 0), (1, 0), (0, 0)))
    pos = jnp.arange(t_len)
    outs = []
    for g, w in enumerate(POOL_WINDOWS):
        sl = slice(g * POOL_GC, (g + 1) * POOL_GC)
        c = cs[:, :, sl]
        lagged = jnp.pad(c, ((0, 0), (w - 1, 0), (0, 0)))[:, :t_len]
        cnt = jnp.minimum(pos + 1, w).astype(f32)[None, :, None]
        outs.append((c[:, 1:] - lagged) / cnt - xf[:, :, sl])
    pooled = jnp.concatenate(outs, axis=-1).reshape(bsz, t_len, POOL_GROUPS, POOL_GC)
    y = jnp.einsum('btgc,gcd->btgd', pooled, w_grp.astype(f32)).reshape(bsz, t_len, POOL_WIDTH)
    return (y * scale.astype(f32)).astype(xp.dtype)


def setup_inputs(seed: int = 0) -> dict:
    key = jax.random.key(seed)
    ks = jax.random.split(key, 16)
    f32 = jnp.float32

    def w(k, shape, fan_in):
        return jax.random.normal(k, shape, f32) * (fan_in ** -0.5)

    def gain(k, shape):
        return 1.0 + 0.05 * jax.random.normal(k, shape, f32)

    return {
        "x": jax.random.normal(ks[0], (BATCH, SEQ, D_MODEL), f32),
        "meta_tokens": jax.random.normal(ks[1], (N_META, D_MODEL), f32),
        "lb_logits": 0.1 * jax.random.normal(ks[2], (DEPTH + 1, HG_WIDTH), f32),
        "norm_mix_g": gain(ks[3], (DEPTH, D_MODEL)),
        "w_in": w(ks[4], (DEPTH, D_MODEL, IN_COLS), D_MODEL),
        "hg_norm_g": gain(ks[5], (DEPTH, HG_WIDTH)),
        "w_pool_grp": w(ks[6], (DEPTH, POOL_GROUPS, POOL_GC, POOL_GC), POOL_GC),
        "pool_scale": 1.0 + 0.1 * jax.random.normal(ks[7], (DEPTH, POOL_WIDTH), f32),
        "w_br_hgrn": w(ks[8], (DEPTH, HG_WIDTH, D_MODEL), HG_WIDTH),
        "w_br_pool": w(ks[9], (DEPTH, POOL_WIDTH, D_MODEL), POOL_WIDTH),
        "w_out": w(ks[10], (DEPTH, D_MODEL, D_MODEL), D_MODEL),
        "norm_ffn_g": gain(ks[11], (DEPTH, D_MODEL)),
        "w_ffn_gate": w(ks[12], (DEPTH, D_MODEL, D_FF), D_MODEL),
        "w_ffn_up": w(ks[13], (DEPTH, D_MODEL, D_FF), D_MODEL),
        "w_ffn_down": w(ks[14], (DEPTH, D_FF, D_MODEL), D_FF),
        "final_norm_g": gain(ks[15], (D_MODEL,)),
    }


def reference(x, meta_tokens, lb_logits, norm_mix_g, w_in, hg_norm_g, w_pool_grp, pool_scale,
              w_br_hgrn, w_br_pool, w_out, norm_ffn_g, w_ffn_gate, w_ffn_up, w_ffn_down,
              final_norm_g):
    f32 = jnp.float32
    bsz = x.shape[0]
    meta = jnp.broadcast_to(meta_tokens.astype(x.dtype)[None], (bsz, N_META, D_MODEL))
    h = jnp.concatenate([meta, x], axis=1)
    t_len = h.shape[1]
    lbs = jnp.cumsum(jax.nn.softmax(lb_logits.astype(f32), axis=0), axis=0)
    for l in range(DEPTH):
        u = rmsnorm(h, norm_mix_g[l])
        proj = u @ w_in[l]
        q_pre, f_pre, i_pre, og_pre, xp, ga, gb = jnp.split(proj, SPLIT_IDX, axis=-1)
        o = hgrn2_mixer(q_pre, f_pre, i_pre, lbs[l])
        o = rmsnorm(o.reshape(bsz, t_len, HG_HEADS, HG_DV),
                    hg_norm_g[l].reshape(HG_HEADS, HG_DV)).reshape(bsz, t_len, HG_WIDTH)
        ya = (o * jax.nn.sigmoid(og_pre.astype(f32))).astype(h.dtype) @ w_br_hgrn[l]
        yb = pool_mixer(xp, w_pool_grp[l], pool_scale[l]) @ w_br_pool[l]
        mixed = jax.nn.sigmoid(ga) * ya + jax.nn.sigmoid(gb) * yb
        h = h + mixed @ w_out[l]
        u = rmsnorm(h, norm_ffn_g[l])
        h = h + (jax.nn.silu(u @ w_ffn_gate[l]) * (u @ w_ffn_up[l])) @ w_ffn_down[l]
    return rmsnorm(h, final_norm_g)[:, N_META:]
```

```python
import functools
import math

import jax
import jax.numpy as jnp
from jax import lax
from jax.experimental import pallas as pl
from jax.experimental.pallas import tpu as pltpu

D_MODEL = 1024
N_META = 16
CHUNK = 64
SUB = 16
N_SUB = CHUNK // SUB
HEADS = 8
DK = 128
HG_WIDTH = HEADS * DK
POOL_WINDOWS = (2, 4, 8, 16)
POOL_GC = 128
POOL_WIDTH = POOL_GC * len(POOL_WINDOWS)
POOL_HIST = 16
EPS = 1e-6
LOG2E = math.log2(math.e)

TM_MIX = 256
TM_FFN = 256
VMEM_LIMIT_BYTES = 56 * 1024 * 1024

C_Q, C_F, C_I, C_OG = 0, HG_WIDTH, 2 * HG_WIDTH, 3 * HG_WIDTH
C_XP = 4 * HG_WIDTH
C_GA = C_XP + POOL_WIDTH
C_GB = C_GA + D_MODEL

F32 = jnp.float32
BF16 = jnp.bfloat16


def _sigmoid(x):
    return 1.0 / (1.0 + jnp.exp(-x))


def _rms_scale(x):
    return lax.rsqrt(jnp.mean(x * x, axis=-1, keepdims=True) + EPS)


def _dot(a, b):
    return jnp.dot(a, b, preferred_element_type=F32)


def _gate_terms(fp, lb):
    t = jnp.exp(-jnp.abs(fp))
    r = 1.0 / (1.0 + t)
    small = t * r
    pos = fp >= 0
    sig_pos = jnp.where(pos, r, small)
    sig_neg = jnp.where(pos, small, r)
    f = lb + (1.0 - lb) * sig_pos
    k = (1.0 - lb) * sig_neg
    return k, jnp.log2(f)


def _block_cumsum(x, n_blocks):
    row = lax.broadcasted_iota(jnp.int32, (8, DK), 0)
    out = []
    for i in range(n_blocks):
        halves = []
        for j in range(SUB // 8):
            v = x[i * SUB + j * 8:i * SUB + (j + 1) * 8]
            for sh in (1, 2, 4):
                v = v + jnp.where(row >= sh, pltpu.roll(v, sh, 0), 0.0)
            if halves:
                v = v + halves[-1][7:8]
            halves.append(v)
        out.extend(halves)
    return jnp.concatenate(out, axis=0)


def _lower_bound(lb_ref):
    l = lb_ref[...].astype(F32)
    e = jnp.exp(l - jnp.max(l, axis=0, keepdims=True))
    return e[0:1] / jnp.sum(e, axis=0, keepdims=True)


def _mixer_kernel(x_ref, meta_ref, lb_ref, g_ref, w_in_ref, hg_ref, wgrp_ref, pscale_ref,
                  w_ha_ref, w_pb_ref, w_out_ref, h_ref,
                  u_s, p_s, o_s, st_s, xp_s):
    t_idx = pl.program_id(1)
    lb_all = _lower_bound(lb_ref)
    g = g_ref[...].astype(F32)

    @pl.when(t_idx == 0)
    def _():
        m = meta_ref[...].astype(F32)
        um = (m * _rms_scale(m) * g).astype(BF16)
        pm = _dot(um, w_in_ref[:, C_F:C_OG])
        xp_s[0:POOL_HIST, :] = _dot(um, w_in_ref[:, C_XP:C_GA])
        for h in range(HEADS):
            cols = slice(h * DK, (h + 1) * DK)
            k, lg2 = _gate_terms(pm[:, cols], lb_all[:, cols])
            v = pm[:, HG_WIDTH + h * DK:HG_WIDTH + (h + 1) * DK]
            c = _block_cumsum(lg2, N_META // SUB)
            kst = k * jnp.exp2(c[N_META - 1:N_META] - c)
            st_s[h] = lax.dot_general(v.astype(BF16), kst.astype(BF16),
                                      (((0,), (0,)), ((), ())), preferred_element_type=F32)

    @pl.when(t_idx > 0)
    def _():
        xp_s[0:POOL_HIST, :] = xp_s[TM_MIX:TM_MIX + POOL_HIST, :]

    x = x_ref[0].astype(F32)
    u_s[...] = (x * _rms_scale(x) * g).astype(BF16)
    p_s[...] = _dot(u_s[...], w_in_ref[:, C_Q:C_OG])

    row16 = lax.broadcasted_iota(jnp.int32, (CHUNK, 1), 0) % SUB
    r64 = lax.broadcasted_iota(jnp.int32, (CHUNK, CHUNK), 0)
    c64 = lax.broadcasted_iota(jnp.int32, (CHUNK, CHUNK), 1)
    mask16 = ((r64 // SUB) % 2 == 1) & (c64 // SUB == r64 // SUB - 1)

    def chunk_body(ci, carry):
        rows = pl.ds(pl.multiple_of(ci * CHUNK, CHUNK), CHUNK)
        for h in range(HEADS):
            cols = slice(h * DK, (h + 1) * DK)
            qp = p_s[rows, C_Q + h * DK:C_Q + (h + 1) * DK]
            fp = p_s[rows, C_F + h * DK:C_F + (h + 1) * DK]
            v = p_s[rows, C_I + h * DK:C_I + (h + 1) * DK]
            q = qp * _sigmoid(qp)
            k, lg2 = _gate_terms(fp, lb_all[:, cols])
            c = _block_cumsum(lg2, N_SUB)
            tot = [c[(i + 1) * SUB - 1:(i + 1) * SUB] for i in range(N_SUB)]
            tot_rows = jnp.concatenate([jnp.broadcast_to(t, (SUB, DK)) for t in tot], axis=0)
            q_e = q * jnp.exp2(c)
            k_r = k * jnp.exp2(tot_rows - c)
            zero_blk = jnp.zeros((SUB, DK), F32)
            blk = lambda a, i: a[i * SUB:(i + 1) * SUB]

            pre = [None, tot[0], tot[0] + tot[1], tot[0] + tot[1] + tot[2]]
            q_in = jnp.concatenate(
                [blk(q_e, 0)] + [blk(q_e, i) * jnp.exp2(pre[i]) for i in range(1, N_SUB)], axis=0)
            post = [tot[1] + tot[2] + tot[3], tot[2] + tot[3], tot[3], None]
            k_st = jnp.concatenate(
                [blk(k_r, i) * jnp.exp2(post[i]) for i in range(N_SUB - 1)] + [blk(k_r, 3)], axis=0)
            q32 = jnp.concatenate([zero_blk, zero_blk, blk(q_e, 2), blk(q_e, 3) * jnp.exp2(tot[2])], axis=0)
            k32 = jnp.concatenate([blk(k_r, 0) * jnp.exp2(tot[1]), blk(k_r, 1), zero_blk, zero_blk], axis=0)
            q16 = jnp.concatenate([zero_blk, blk(q_e, 1), zero_blk, blk(q_e, 3)], axis=0)
            k16 = jnp.concatenate([blk(k_r, 0), zero_blk, blk(k_r, 2), zero_blk], axis=0)
            nt = (((1,), (1,)), ((), ()))
            a32 = lax.dot_general(q32.astype(BF16), k32.astype(BF16), nt, preferred_element_type=F32)
            a16 = lax.dot_general(q16.astype(BF16), k16.astype(BF16), nt, preferred_element_type=F32)
            a = a32 + jnp.where(mask16, a16, 0.0)

            st = st_s[h]
            v_bf = v.astype(BF16)
            o = lax.dot_general(q_in.astype(BF16), st.astype(BF16), nt, preferred_element_type=F32)
            o = o + _dot(a.astype(BF16), v_bf)

            c4 = c.reshape(N_SUB, SUB, DK)
            k4 = k.reshape(N_SUB, SUB, DK)
            v4 = v.reshape(N_SUB, SUB, DK)
            for s in range(SUB):
                bc = lambda a4: jnp.broadcast_to(a4[:, s:s + 1, :], (N_SUB, SUB, DK)).reshape(CHUNK, DK)
                w = jnp.exp2(jnp.minimum(c - bc(c4), 0.0))
                a_s = jnp.sum(q * bc(k4) * w, axis=-1, keepdims=True)
                a_s = jnp.where(row16 >= s, a_s, 0.0)
                o = o + a_s * bc(v4)
            o_s[rows, cols] = o

            b_last = pre[3] + tot[3]
            st_s[h] = st * jnp.exp2(b_last) + lax.dot_general(
                v_bf, k_st.astype(BF16), (((0,), (0,)), ((), ())), preferred_element_type=F32)
        return carry

    lax.fori_loop(0, TM_MIX // CHUNK, chunk_body, 0)

    hg = hg_ref[...].astype(F32)
    og = _dot(u_s[...], w_in_ref[:, C_OG:C_XP])
    parts = []
    for h in range(HEADS):
        cols = slice(h * DK, (h + 1) * DK)
        oh = o_s[:, cols]
        parts.append(oh * _rms_scale(oh) * hg[:, cols])
    on = jnp.concatenate(parts, axis=-1)
    ya = _dot((on * _sigmoid(og)).astype(BF16), w_ha_ref[...])

    xp_s[POOL_HIST:POOL_HIST + TM_MIX, :] = _dot(u_s[...], w_in_ref[:, C_XP:C_GA])
    ys = []
    for gi, w in enumerate(POOL_WINDOWS):
        cols = slice(gi * POOL_GC, (gi + 1) * POOL_GC)
        cur = xp_s[POOL_HIST:POOL_HIST + TM_MIX, cols]
        acc = cur
        for j in range(1, w):
            acc = acc + xp_s[POOL_HIST - j:POOL_HIST - j + TM_MIX, cols]
        pooled = acc * (1.0 / w) - cur
        ys.append(_dot(pooled.astype(BF16), wgrp_ref[gi]))
    y = jnp.concatenate(ys, axis=-1) * pscale_ref[...].astype(F32)
    yb = _dot(y.astype(BF16), w_pb_ref[...])

    ga = _dot(u_s[...], w_in_ref[:, C_GA:C_GB])
    gb = _dot(u_s[...], w_in_ref[:, C_GB:C_GB + D_MODEL])
    mixed = _sigmoid(ga) * ya + _sigmoid(gb) * yb
    h_ref[0] = x_ref[0].astype(F32) + _dot(mixed.astype(BF16), w_out_ref[...])


def _ffn_kernel(h_ref, g_ref, wg_ref, wu_ref, wd_ref, gf_ref, out_ref):
    h = h_ref[...].astype(F32)
    u = (h * _rms_scale(h) * g_ref[...].astype(F32)).astype(BF16)
    gate = _dot(u, wg_ref[...])
    up = _dot(u, wu_ref[...])
    act = (gate * _sigmoid(gate) * up).astype(BF16)
    h2 = h + _dot(act, wd_ref[...])
    out_ref[...] = (h2 * _rms_scale(h2) * gf_ref[...].astype(F32)).astype(out_ref.dtype)


def _const_spec(shape):
    nd = len(shape)
    return pl.BlockSpec(shape, lambda *_: (0,) * nd)


def kernel(x, meta_tokens, lb_logits, norm_mix_g, w_in, hg_norm_g, w_pool_grp, pool_scale, w_br_hgrn, w_br_pool, w_out, norm_ffn_g, w_ffn_gate, w_ffn_up, w_ffn_down, final_norm_g):
    bsz, seq, d = x.shape
    assert d == D_MODEL and seq % TM_MIX == 0 and (bsz * seq) % TM_FFN == 0
    assert w_in.shape[0] == 1, "single-layer block"
    in_cols = w_in.shape[-1]
    d_ff = w_ffn_gate.shape[-1]

    row = lambda a: a.reshape(1, -1)
    mixer_args = (
        x, meta_tokens, lb_logits, row(norm_mix_g[0]), w_in[0].astype(BF16), row(hg_norm_g[0]),
        w_pool_grp[0].astype(BF16), row(pool_scale[0]), w_br_hgrn[0].astype(BF16),
        w_br_pool[0].astype(BF16), w_out[0].astype(BF16))
    in_specs = [pl.BlockSpec((1, TM_MIX, d), lambda b, t: (b, t, 0))]
    in_specs += [_const_spec(a.shape) for a in mixer_args[1:]]
    h1 = pl.pallas_call(
        _mixer_kernel,
        grid=(bsz, seq // TM_MIX),
        in_specs=in_specs,
        out_specs=pl.BlockSpec((1, TM_MIX, d), lambda b, t: (b, t, 0)),
        out_shape=jax.ShapeDtypeStruct((bsz, seq, d), F32),
        scratch_shapes=[
            pltpu.VMEM((TM_MIX, d), BF16),
            pltpu.VMEM((TM_MIX, 3 * HG_WIDTH), F32),
            pltpu.VMEM((TM_MIX, HG_WIDTH), F32),
            pltpu.VMEM((HEADS, DK, DK), F32),
            pltpu.VMEM((POOL_HIST + TM_MIX, POOL_WIDTH), F32),
        ],
        compiler_params=pltpu.CompilerParams(
            dimension_semantics=("arbitrary", "arbitrary"),
            vmem_limit_bytes=VMEM_LIMIT_BYTES),
        name="mixer",
    )(*mixer_args)

    n_tok = bsz * seq
    ffn_args = (h1.reshape(n_tok, d), row(norm_ffn_g[0]), w_ffn_gate[0].astype(BF16),
                w_ffn_up[0].astype(BF16), w_ffn_down[0].astype(BF16), row(final_norm_g))
    out = pl.pallas_call(
        _ffn_kernel,
        grid=(n_tok // TM_FFN,),
        in_specs=[pl.BlockSpec((TM_FFN, d), lambda i: (i, 0))] + [_const_spec(a.shape) for a in ffn_args[1:]],
        out_specs=pl.BlockSpec((TM_FFN, d), lambda i: (i, 0)),
        out_shape=jax.ShapeDtypeStruct((n_tok, d), x.dtype),
        compiler_params=pltpu.CompilerParams(
            dimension_semantics=("arbitrary",),
            vmem_limit_bytes=VMEM_LIMIT_BYTES),
        name="ffn",
    )(*ffn_args)
    return out.reshape(bsz, seq, d)
```

```python
import functools

import jax
import jax.numpy as jnp
from jax import lax
from jax.experimental import pallas as pl
from jax.experimental.pallas import tpu as pltpu

D_MODEL = 1024
N_META = 16
CHUNK = 64
SUB = 16
N_SUB = CHUNK // SUB
HEADS = 8
DK = 128
LANES = 128
SUBLANES = 8
HG_WIDTH = HEADS * DK
POOL_WINDOWS = (2, 4, 8, 16)
POOL_GC = 128
POOL_WIDTH = POOL_GC * len(POOL_WINDOWS)
POOL_HIST = 16
EPS = 1e-6

TM_MIX = 256
N_CHUNKS = TM_MIX // CHUNK
TM_FFN = 256
VMEM_LIMIT_BYTES = 56 * 1024 * 1024

QFI_COLS = 3 * HG_WIDTH
GROUP_HEADS = 2
N_GROUPS = HEADS // GROUP_HEADS
GROUP_COLS = 3 * GROUP_HEADS * DK
G_Q, G_F, G_I = 0, GROUP_HEADS * DK, 2 * GROUP_HEADS * DK
REST_COLS = HG_WIDTH + POOL_WIDTH + 2 * D_MODEL
R_OG, R_XP = 0, HG_WIDTH
R_GA = R_XP + POOL_WIDTH
R_GB = R_GA + D_MODEL
REST_PART = REST_COLS // N_GROUPS
GROUP_STRIDE = GROUP_COLS + REST_PART
N_RECT_SEG = 3
N_SEG = N_RECT_SEG + 1
SEG_W = N_SEG * DK

F32 = jnp.float32
BF16 = jnp.bfloat16
NT_DIMS = (((1,), (1,)), ((), ()))
TN_DIMS = (((0,), (0,)), ((), ()))


def _sigmoid(x):
    return 1.0 / (1.0 + jnp.exp(-x))


def _rms_scale(x):
    return lax.rsqrt(jnp.mean(x * x, axis=-1, keepdims=True) + EPS)


def _dot(a, b):
    return jnp.dot(a, b, preferred_element_type=F32)


def _gate_terms(fp, lb):
    f = lb + (1.0 - lb) / (1.0 + jnp.exp(-fp))
    return 1.0 - f, jnp.log2(f)


def _block_cumsum(x, n_blocks):
    row = lax.broadcasted_iota(jnp.int32, (SUBLANES, DK), 0)
    out = []
    for i in range(n_blocks):
        halves = []
        for j in range(SUB // SUBLANES):
            v = x[i * SUB + j * SUBLANES:i * SUB + (j + 1) * SUBLANES]
            for sh in (1, 2, 4):
                v = v + jnp.where(row >= sh, pltpu.roll(v, sh, 0), 0.0)
            if halves:
                v = v + halves[-1][SUBLANES - 1:SUBLANES]
            halves.append(v)
        out.extend(halves)
    return jnp.concatenate(out, axis=0)


def _lower_bound(lb_ref):
    l = lb_ref[...].astype(F32)
    e = jnp.exp(l - jnp.max(l, axis=0, keepdims=True))
    return e[0:1] / jnp.sum(e, axis=0, keepdims=True)


def _mixer_kernel(x_ref, meta_ref, lb_ref, g_ref, w_all_ref, hg_ref, wgrp_ref,
                  pscale_ref, w_ha_ref, w_pb_ref, w_out_ref, h_ref,
                  p0_s, p1_s, p2_s, p3_s, r_s, qin_s, kst_s, v_s, qcat_s, kcat_s, q_s, c_s, cs_s, dec_s,
                  o_s, st_s, xp_s, *, tiles_per_seq):
    t_idx = pl.program_id(0) % tiles_per_seq
    lb_all = _lower_bound(lb_ref)
    g = g_ref[...].astype(F32)

    def normed(x):
        return (x * _rms_scale(x) * g).astype(BF16)

    @pl.when(t_idx == 0)
    def _():
        um = normed(meta_ref[...].astype(F32))
        pm = [_dot(um, w_all_ref[:, gi * GROUP_STRIDE:gi * GROUP_STRIDE + GROUP_COLS])
              for gi in range(N_GROUPS)]
        part, within = divmod(R_XP, REST_PART)
        assert within + POOL_WIDTH <= REST_PART
        xp0 = part * GROUP_STRIDE + GROUP_COLS + within
        xp_s[0:POOL_HIST, :] = _dot(um, w_all_ref[:, xp0:xp0 + POOL_WIDTH])
        for h in range(HEADS):
            gi, off = divmod(h, GROUP_HEADS)
            cols = slice(h * DK, (h + 1) * DK)
            k, lg2 = _gate_terms(pm[gi][:, G_F + off * DK:G_F + (off + 1) * DK], lb_all[:, cols])
            v = pm[gi][:, G_I + off * DK:G_I + (off + 1) * DK]
            c = _block_cumsum(lg2, N_META // SUB)
            kst = k * jnp.exp2(c[N_META - 1:N_META] - c)
            st_s[h] = lax.dot_general(v.astype(BF16), kst.astype(BF16), TN_DIMS,
                                      preferred_element_type=F32)

    @pl.when(t_idx > 0)
    def _():
        xp_s[0:POOL_HIST, :] = xp_s[TM_MIX:TM_MIX + POOL_HIST, :]

    x = x_ref[0].astype(F32)
    proj = _dot(normed(x), w_all_ref[...])
    p_s = (p0_s, p1_s, p2_s, p3_s)
    for gi in range(N_GROUPS):
        p_s[gi][...] = proj[:, gi * GROUP_STRIDE:gi * GROUP_STRIDE + GROUP_COLS]
        r_s[:, gi * REST_PART:(gi + 1) * REST_PART] = (
            proj[:, gi * GROUP_STRIDE + GROUP_COLS:(gi + 1) * GROUP_STRIDE])

    zero_blk = jnp.zeros((SUB, DK), F32)
    for gi in range(N_GROUPS):
        for ci, off in [(ci, off) for ci in range(N_CHUNKS) for off in range(GROUP_HEADS)]:
            h = gi * GROUP_HEADS + off
            rows = slice(ci * CHUNK, (ci + 1) * CHUNK)
            cols = slice(h * DK, (h + 1) * DK)
            qp = p_s[gi][rows, G_Q + off * DK:G_Q + (off + 1) * DK]
            fp = p_s[gi][rows, G_F + off * DK:G_F + (off + 1) * DK]
            v = p_s[gi][rows, G_I + off * DK:G_I + (off + 1) * DK]
            q = qp * _sigmoid(qp)
            k, lg2 = _gate_terms(fp, lb_all[:, cols])
            c = _block_cumsum(lg2, N_SUB)
            tot = [c[(i + 1) * SUB - 1:(i + 1) * SUB] for i in range(N_SUB)]
            tot_rows = jnp.concatenate([jnp.broadcast_to(t, (SUB, DK)) for t in tot], axis=0)
            q_e = q * jnp.exp2(c)
            k_r = k * jnp.exp2(tot_rows - c)
            qe = [q_e[i * SUB:(i + 1) * SUB] for i in range(N_SUB)]
            kr = [k_r[i * SUB:(i + 1) * SUB] for i in range(N_SUB)]

            pre = [None, tot[0], tot[0] + tot[1], tot[0] + tot[1] + tot[2]]
            q_in = jnp.concatenate([qe[0]] + [qe[i] * jnp.exp2(pre[i]) for i in range(1, N_SUB)], axis=0)
            post = [tot[1] + tot[2] + tot[3], tot[2] + tot[3], tot[3], None]
            k_st = jnp.concatenate([kr[i] * jnp.exp2(post[i]) for i in range(N_SUB - 1)] + [kr[3]], axis=0)
            z = zero_blk
            q_segs = [jnp.concatenate(b, axis=0) for b in (
                [z, qe[1], z, z], [z, z, z, qe[3]], [z, z, qe[2], qe[3] * jnp.exp2(tot[2])])]
            k_segs = [jnp.concatenate(b, axis=0) for b in (
                [kr[0], z, z, z], [z, z, kr[2], z], [kr[0] * jnp.exp2(tot[1]), kr[1], z, z])]
            z8 = jnp.zeros((SUBLANES, DK), F32)
            q_half, k_half = [], []
            for i in range(N_SUB):
                lo = slice(i * SUB, i * SUB + SUBLANES)
                hi = slice(i * SUB + SUBLANES, (i + 1) * SUB)
                c_mid = c[i * SUB + SUBLANES - 1:i * SUB + SUBLANES]
                q_half += [z8, q[hi] * jnp.exp2(c[hi] - c_mid)]
                k_half += [k[lo] * jnp.exp2(c_mid - c[lo]), z8]
            q_segs.append(jnp.concatenate(q_half, axis=0))
            k_segs.append(jnp.concatenate(k_half, axis=0))

            q_s[rows, cols] = q
            c_s[rows, cols] = c
            cs_s[rows, cols] = c - jnp.log2(k)
            qin_s[rows, cols] = q_in.astype(BF16)
            kst_s[rows, cols] = k_st.astype(BF16)
            v_s[rows, cols] = v.astype(BF16)
            for j in range(N_SEG):
                seg = slice(h * SEG_W + j * DK, h * SEG_W + (j + 1) * DK)
                qcat_s[rows, seg] = q_segs[j].astype(BF16)
                kcat_s[rows, seg] = k_segs[j].astype(BF16)
            dec_s[ci * SUBLANES:ci * SUBLANES + 1, cols] = jnp.exp2(pre[3] + tot[3])

    row8 = lax.broadcasted_iota(jnp.int32, (SUBLANES, CHUNK), 0)
    lane8 = lax.broadcasted_iota(jnp.int32, (SUBLANES, CHUNK), 1)
    r64 = lax.broadcasted_iota(jnp.int32, (CHUNK, CHUNK), 0)
    c64 = lax.broadcasted_iota(jnp.int32, (CHUNK, CHUNK), 1)
    same_block = (r64 // SUB) == (c64 // SUB)

    def chunk_body(ci, carry):
        base = pl.multiple_of(ci * CHUNK, CHUNK)
        rows = pl.ds(base, CHUNK)
        dec_row = pl.ds(pl.multiple_of(ci * SUBLANES, SUBLANES), 1)
        head_cols = [slice(h * DK, (h + 1) * DK) for h in range(HEADS)]
        a_offs = []
        for h in range(HEADS):
            segs_rect = slice(h * SEG_W, h * SEG_W + N_RECT_SEG * DK)
            seg_half = slice(h * SEG_W + N_RECT_SEG * DK, (h + 1) * SEG_W)
            a_off = lax.dot_general(qcat_s[rows, segs_rect], kcat_s[rows, segs_rect], NT_DIMS,
                                    preferred_element_type=F32)
            a_half = lax.dot_general(qcat_s[rows, seg_half], kcat_s[rows, seg_half], NT_DIMS,
                                     preferred_element_type=F32)
            a_offs.append(a_off + jnp.where(same_block, a_half, 0.0))

        scores = []
        for h in range(HEADS):
            cols = head_cols[h]
            a_off = a_offs[h]
            groups = []
            for r0 in range(0, CHUNK, SUBLANES):
                grp = pl.ds(pl.multiple_of(base + r0, SUBLANES), SUBLANES)
                q8 = q_s[grp, cols]
                c8 = c_s[grp, cols]
                cs8 = cs_s[grp, cols]
                a8 = a_off[r0:r0 + SUBLANES]
                for s in range(SUBLANES):
                    a_s = jnp.sum(q8 * jnp.exp2(c8 - cs8[s:s + 1]), axis=-1, keepdims=True)
                    a8 = jnp.where((lane8 == r0 + s) & (row8 >= s), a_s, a8)
                groups.append(a8)
            scores.append(jnp.concatenate(groups, axis=0).astype(BF16))

        for h in range(HEADS):
            cols = head_cols[h]
            st = st_s[h]
            v_bf = v_s[rows, cols]
            o = lax.dot_general(qin_s[rows, cols], st.astype(BF16), NT_DIMS, preferred_element_type=F32)
            o_s[rows, cols] = o + _dot(scores[h], v_bf)
            st_s[h] = st * dec_s[dec_row, cols] + lax.dot_general(
                v_bf, kst_s[rows, cols], TN_DIMS, preferred_element_type=F32)
        return carry

    lax.fori_loop(0, N_CHUNKS, chunk_body, 0)

    hg = hg_ref[...].astype(F32)
    parts = []
    for h in range(HEADS):
        cols = slice(h * DK, (h + 1) * DK)
        oh = o_s[:, cols]
        parts.append(oh * _rms_scale(oh) * hg[:, cols])
    on = jnp.concatenate(parts, axis=-1)
    ya = _dot((on * _sigmoid(r_s[:, R_OG:R_XP])).astype(BF16), w_ha_ref[...])

    xp_s[POOL_HIST:POOL_HIST + TM_MIX, :] = r_s[:, R_XP:R_GA]
    ys = []
    for gi, w in enumerate(POOL_WINDOWS):
        cols = slice(gi * POOL_GC, (gi + 1) * POOL_GC)
        cur = xp_s[POOL_HIST:POOL_HIST + TM_MIX, cols]
        acc = cur
        for j in range(1, w):
            acc = acc + xp_s[POOL_HIST - j:POOL_HIST - j + TM_MIX, cols]
        pooled = acc * (1.0 / w) - cur
        ys.append(_dot(pooled.astype(BF16), wgrp_ref[gi]))
    y = jnp.concatenate(ys, axis=-1) * pscale_ref[...].astype(F32)
    yb = _dot(y.astype(BF16), w_pb_ref[...])

    mixed = _sigmoid(r_s[:, R_GA:R_GB]) * ya + _sigmoid(r_s[:, R_GB:REST_COLS]) * yb
    h_ref[0] = x_ref[0].astype(F32) + _dot(mixed.astype(BF16), w_out_ref[...])


def _ffn_kernel(h_ref, g_ref, wg_ref, wu_ref, wd_ref, gf_ref, out_ref):
    h = h_ref[...].astype(F32)
    u = (h * _rms_scale(h) * g_ref[...].astype(F32)).astype(BF16)
    gate = _dot(u, wg_ref[...])
    up = _dot(u, wu_ref[...])
    act = (gate * _sigmoid(gate) * up).astype(BF16)
    h2 = h + _dot(act, wd_ref[...])
    out_ref[...] = (h2 * _rms_scale(h2) * gf_ref[...].astype(F32)).astype(out_ref.dtype)


def _const_spec(shape):
    nd = len(shape)
    return pl.BlockSpec(shape, lambda *_: (0,) * nd, pipeline_mode=pl.Buffered(1))


def kernel(x, meta_tokens, lb_logits, norm_mix_g, w_in, hg_norm_g, w_pool_grp, pool_scale, w_br_hgrn, w_br_pool, w_out, norm_ffn_g, w_ffn_gate, w_ffn_up, w_ffn_down, final_norm_g):
    bsz, seq, d = x.shape
    assert d == D_MODEL and seq % TM_MIX == 0 and (bsz * seq) % TM_FFN == 0
    assert w_in.shape[0] == 1, "single-layer block"
    assert w_in.shape[-1] == QFI_COLS + REST_COLS
    tiles_per_seq = seq // TM_MIX
    n_tiles = bsz * tiles_per_seq

    row = lambda a: a.reshape(1, -1)
    w_in_bf = w_in[0].astype(BF16)
    w_qfi = w_in_bf[:, :QFI_COLS].reshape(d, 3, N_GROUPS, GROUP_HEADS * DK).transpose(2, 0, 1, 3)
    w_qfi = w_qfi.reshape(N_GROUPS, d, GROUP_COLS)
    w_rest = w_in_bf[:, QFI_COLS:].reshape(d, N_GROUPS, REST_PART).transpose(1, 0, 2)
    w_all = jnp.concatenate([w_qfi, w_rest], axis=-1).transpose(1, 0, 2).reshape(d, N_GROUPS * GROUP_STRIDE)
    mixer_consts = (
        meta_tokens, lb_logits, row(norm_mix_g[0]), w_all,
        row(hg_norm_g[0]), w_pool_grp[0].astype(BF16), row(pool_scale[0]), w_br_hgrn[0].astype(BF16),
        w_br_pool[0].astype(BF16), w_out[0].astype(BF16))
    tile_spec = pl.BlockSpec((1, TM_MIX, d), lambda i: (i, 0, 0))
    h1 = pl.pallas_call(
        functools.partial(_mixer_kernel, tiles_per_seq=tiles_per_seq),
        grid=(n_tiles,),
        in_specs=[tile_spec] + [_const_spec(a.shape) for a in mixer_consts],
        out_specs=tile_spec,
        out_shape=jax.ShapeDtypeStruct((n_tiles, TM_MIX, d), F32),
        scratch_shapes=[
            pltpu.VMEM((TM_MIX, GROUP_COLS), F32),
            pltpu.VMEM((TM_MIX, GROUP_COLS), F32),
            pltpu.VMEM((TM_MIX, GROUP_COLS), F32),
            pltpu.VMEM((TM_MIX, GROUP_COLS), F32),
            pltpu.VMEM((TM_MIX, REST_COLS), F32),
            pltpu.VMEM((TM_MIX, HG_WIDTH), BF16),
            pltpu.VMEM((TM_MIX, HG_WIDTH), BF16),
            pltpu.VMEM((TM_MIX, HG_WIDTH), BF16),
            pltpu.VMEM((TM_MIX, HEADS * SEG_W), BF16),
            pltpu.VMEM((TM_MIX, HEADS * SEG_W), BF16),
            pltpu.VMEM((TM_MIX, HG_WIDTH), F32),
            pltpu.VMEM((TM_MIX, HG_WIDTH), F32),
            pltpu.VMEM((TM_MIX, HG_WIDTH), F32),
            pltpu.VMEM((N_CHUNKS * SUBLANES, HG_WIDTH), F32),
            pltpu.VMEM((TM_MIX, HG_WIDTH), F32),
            pltpu.VMEM((HEADS, DK, DK), F32),
            pltpu.VMEM((POOL_HIST + TM_MIX, POOL_WIDTH), F32),
        ],
        compiler_params=pltpu.CompilerParams(
            dimension_semantics=("arbitrary",),
            vmem_limit_bytes=VMEM_LIMIT_BYTES),
        name="mixer",
    )(x.reshape(n_tiles, TM_MIX, d), *mixer_consts)

    n_tok = bsz * seq
    ffn_consts = (row(norm_ffn_g[0]), w_ffn_gate[0].astype(BF16), w_ffn_up[0].astype(BF16),
                  w_ffn_down[0].astype(BF16), row(final_norm_g))
    out = pl.pallas_call(
        _ffn_kernel,
        grid=(n_tok // TM_FFN,),
        in_specs=[pl.BlockSpec((TM_FFN, d), lambda i: (i, 0))] + [_const_spec(a.shape) for a in ffn_consts],
        out_specs=pl.BlockSpec((TM_FFN, d), lambda i: (i, 0)),
        out_shape=jax.ShapeDtypeStruct((n_tok, d), x.dtype),
        compiler_params=pltpu.CompilerParams(
            dimension_semantics=("arbitrary",),
            vmem_limit_bytes=VMEM_LIMIT_BYTES),
        name="ffn",
    )(h1.reshape(n_tok, d), *ffn_consts)
    return out.reshape(bsz, seq, d)
```

```python
import functools

import jax
import jax.numpy as jnp
from jax import lax
from jax.experimental import pallas as pl
from jax.experimental.pallas import tpu as pltpu

D_MODEL = 1024
N_META = 16
CHUNK = 64
SUB = 16
N_SUB = CHUNK // SUB
HEADS = 8
DK = 128
LANES = 128
SUBLANES = 8
HG_WIDTH = HEADS * DK
POOL_WINDOWS = (2, 4, 8, 16)
POOL_GC = 128
POOL_WIDTH = POOL_GC * len(POOL_WINDOWS)
POOL_HIST = 16
EPS = 1e-6

TM_MIX = 256
N_CHUNKS = TM_MIX // CHUNK
TM_FFN = 256
VMEM_LIMIT_BYTES = 56 * 1024 * 1024

QFI_COLS = 3 * HG_WIDTH
GROUP_HEADS = 2
N_GROUPS = HEADS // GROUP_HEADS
GROUP_COLS = 3 * GROUP_HEADS * DK
G_Q, G_F, G_I = 0, GROUP_HEADS * DK, 2 * GROUP_HEADS * DK
REST_COLS = HG_WIDTH + POOL_WIDTH + 2 * D_MODEL
R_OG, R_XP = 0, HG_WIDTH
R_GA = R_XP + POOL_WIDTH
R_GB = R_GA + D_MODEL
REST_PART = REST_COLS // N_GROUPS
GROUP_STRIDE = GROUP_COLS + REST_PART
N_RECT_SEG = 3
N_SEG = N_RECT_SEG + 1
SEG_W = N_SEG * DK

F32 = jnp.float32
BF16 = jnp.bfloat16
NT_DIMS = (((1,), (1,)), ((), ()))
TN_DIMS = (((0,), (0,)), ((), ()))


def _sigmoid(x):
    return 1.0 / (1.0 + jnp.exp(-x))


def _rms_scale(x):
    return lax.rsqrt(jnp.mean(x * x, axis=-1, keepdims=True) + EPS)


def _dot(a, b):
    return jnp.dot(a, b, preferred_element_type=F32)


def _gate_terms(fp, lb):
    f = lb + (1.0 - lb) / (1.0 + jnp.exp(-fp))
    return 1.0 - f, jnp.log2(f)


def _block_cumsum(x, n_blocks):
    row = lax.broadcasted_iota(jnp.int32, (SUBLANES, DK), 0)
    out = []
    for i in range(n_blocks):
        halves = []
        for j in range(SUB // SUBLANES):
            v = x[i * SUB + j * SUBLANES:i * SUB + (j + 1) * SUBLANES]
            for sh in (1, 2, 4):
                v = v + jnp.where(row >= sh, pltpu.roll(v, sh, 0), 0.0)
            if halves:
                v = v + halves[-1][SUBLANES - 1:SUBLANES]
            halves.append(v)
        out.extend(halves)
    return jnp.concatenate(out, axis=0)


def _lower_bound(lb_ref):
    l = lb_ref[...].astype(F32)
    e = jnp.exp(l - jnp.max(l, axis=0, keepdims=True))
    return e[0:1] / jnp.sum(e, axis=0, keepdims=True)


def _mixer_kernel(x_ref, meta_ref, lb_ref, g_ref, w_all_ref, hg_ref, wgrp_ref,
                  pscale_ref, w_ha_ref, w_pb_ref, w_out_ref, h_ref,
                  p0_s, p1_s, p2_s, p3_s, r_s, qin_s, kst_s, v_s, qcat_s, kcat_s, q_s, c_s, cs_s, dec_s,
                  o_s, st_s, xp_s, *, tiles_per_seq):
    t_idx = pl.program_id(0) % tiles_per_seq
    lb_all = _lower_bound(lb_ref)
    g = g_ref[...].astype(F32)

    def normed(x):
        return (x * _rms_scale(x) * g).astype(BF16)

    @pl.when(t_idx == 0)
    def _():
        um = normed(meta_ref[...].astype(F32))
        pm = [_dot(um, w_all_ref[:, gi * GROUP_STRIDE:gi * GROUP_STRIDE + GROUP_COLS])
              for gi in range(N_GROUPS)]
        part, within = divmod(R_XP, REST_PART)
        assert within + POOL_WIDTH <= REST_PART
        xp0 = part * GROUP_STRIDE + GROUP_COLS + within
        xp_s[0:POOL_HIST, :] = _dot(um, w_all_ref[:, xp0:xp0 + POOL_WIDTH])
        for h in range(HEADS):
            gi, off = divmod(h, GROUP_HEADS)
            cols = slice(h * DK, (h + 1) * DK)
            k, lg2 = _gate_terms(pm[gi][:, G_F + off * DK:G_F + (off + 1) * DK], lb_all[:, cols])
            v = pm[gi][:, G_I + off * DK:G_I + (off + 1) * DK]
            c = _block_cumsum(lg2, N_META // SUB)
            kst = k * jnp.exp2(c[N_META - 1:N_META] - c)
            st_s[h] = lax.dot_general(v.astype(BF16), kst.astype(BF16), TN_DIMS,
                                      preferred_element_type=F32)

    @pl.when(t_idx > 0)
    def _():
        xp_s[0:POOL_HIST, :] = xp_s[TM_MIX:TM_MIX + POOL_HIST, :]

    x = x_ref[0].astype(F32)
    proj = _dot(normed(x), w_all_ref[...])
    p_s = (p0_s, p1_s, p2_s, p3_s)
    for gi in range(N_GROUPS):
        p_s[gi][...] = proj[:, gi * GROUP_STRIDE:gi * GROUP_STRIDE + GROUP_COLS]
        r_s[:, gi * REST_PART:(gi + 1) * REST_PART] = (
            proj[:, gi * GROUP_STRIDE + GROUP_COLS:(gi + 1) * GROUP_STRIDE])

    zero_blk = jnp.zeros((SUB, DK), F32)
    for gi in range(N_GROUPS):
        for ci, off in [(ci, off) for ci in range(N_CHUNKS) for off in range(GROUP_HEADS)]:
            h = gi * GROUP_HEADS + off
            rows = slice(ci * CHUNK, (ci + 1) * CHUNK)
            cols = slice(h * DK, (h + 1) * DK)
            qp = p_s[gi][rows, G_Q + off * DK:G_Q + (off + 1) * DK]
            fp = p_s[gi][rows, G_F + off * DK:G_F + (off + 1) * DK]
            v = p_s[gi][rows, G_I + off * DK:G_I + (off + 1) * DK]
            q = qp * _sigmoid(qp)
            k, lg2 = _gate_terms(fp, lb_all[:, cols])
            c = _block_cumsum(lg2, N_SUB)
            tot = [c[(i + 1) * SUB - 1:(i + 1) * SUB] for i in range(N_SUB)]
            tot_rows = jnp.concatenate([jnp.broadcast_to(t, (SUB, DK)) for t in tot], axis=0)
            q_e = q * jnp.exp2(c)
            k_r = k * jnp.exp2(tot_rows - c)
            qe = [q_e[i * SUB:(i + 1) * SUB] for i in range(N_SUB)]
            kr = [k_r[i * SUB:(i + 1) * SUB] for i in range(N_SUB)]

            pre = [None, tot[0], tot[0] + tot[1], tot[0] + tot[1] + tot[2]]
            q_in = jnp.concatenate([qe[0]] + [qe[i] * jnp.exp2(pre[i]) for i in range(1, N_SUB)], axis=0)
            post = [tot[1] + tot[2] + tot[3], tot[2] + tot[3], tot[3], None]
            k_st = jnp.concatenate([kr[i] * jnp.exp2(post[i]) for i in range(N_SUB - 1)] + [kr[3]], axis=0)
            z = zero_blk
            q_segs = [jnp.concatenate(b, axis=0) for b in (
                [z, qe[1], z, z], [z, z, z, qe[3]], [z, z, qe[2], qe[3] * jnp.exp2(tot[2])])]
            k_segs = [jnp.concatenate(b, axis=0) for b in (
                [kr[0], z, z, z], [z, z, kr[2], z], [kr[0] * jnp.exp2(tot[1]), kr[1], z, z])]
            z8 = jnp.zeros((SUBLANES, DK), F32)
            q_half, k_half = [], []
            for i in range(N_SUB):
                lo = slice(i * SUB, i * SUB + SUBLANES)
                hi = slice(i * SUB + SUBLANES, (i + 1) * SUB)
                c_mid = c[i * SUB + SUBLANES - 1:i * SUB + SUBLANES]
                q_half += [z8, q[hi] * jnp.exp2(c[hi] - c_mid)]
                k_half += [k[lo] * jnp.exp2(c_mid - c[lo]), z8]
            q_segs.append(jnp.concatenate(q_half, axis=0))
            k_segs.append(jnp.concatenate(k_half, axis=0))

            q_s[rows, cols] = q
            c_s[rows, cols] = c
            cs_s[rows, cols] = c - jnp.log2(k)
            qin_s[rows, cols] = q_in.astype(BF16)
            kst_s[rows, cols] = k_st.astype(BF16)
            v_s[rows, cols] = v.astype(BF16)
            for j in range(N_SEG):
                seg = slice(h * SEG_W + j * DK, h * SEG_W + (j + 1) * DK)
                qcat_s[rows, seg] = q_segs[j].astype(BF16)
                kcat_s[rows, seg] = k_segs[j].astype(BF16)
            dec_s[ci * SUBLANES:ci * SUBLANES + 1, cols] = jnp.exp2(pre[3] + tot[3])

    r64 = lax.broadcasted_iota(jnp.int32, (CHUNK, CHUNK), 0)
    c64 = lax.broadcasted_iota(jnp.int32, (CHUNK, CHUNK), 1)
    same_block = (r64 // SUB) == (c64 // SUB)
    row8 = lax.broadcasted_iota(jnp.int32, (SUBLANES, CHUNK), 0)
    lane8 = lax.broadcasted_iota(jnp.int32, (SUBLANES, CHUNK), 1)

    def chunk_step(ci):
        base = ci * CHUNK
        rows = slice(base, base + CHUNK)
        dec_row = slice(ci * SUBLANES, ci * SUBLANES + 1)
        head_cols = [slice(h * DK, (h + 1) * DK) for h in range(HEADS)]
        a_offs = []
        for h in range(HEADS):
            segs_rect = slice(h * SEG_W, h * SEG_W + N_RECT_SEG * DK)
            seg_half = slice(h * SEG_W + N_RECT_SEG * DK, (h + 1) * SEG_W)
            a_off = lax.dot_general(qcat_s[rows, segs_rect], kcat_s[rows, segs_rect], NT_DIMS,
                                    preferred_element_type=F32)
            a_half = lax.dot_general(qcat_s[rows, seg_half], kcat_s[rows, seg_half], NT_DIMS,
                                     preferred_element_type=F32)
            a_offs.append(a_off + jnp.where(same_block, a_half, 0.0))

        scores = []
        for h in range(HEADS):
            cols = head_cols[h]
            groups = []
            for r0 in range(0, CHUNK, SUBLANES):
                grp = slice(base + r0, base + r0 + SUBLANES)
                q8 = q_s[grp, cols]
                c8 = c_s[grp, cols]
                cs8 = cs_s[grp, cols]
                a8 = a_offs[h][r0:r0 + SUBLANES]
                for s in range(SUBLANES):
                    a_s = jnp.sum(q8 * jnp.exp2(c8 - cs8[s:s + 1]), axis=-1, keepdims=True)
                    a8 = jnp.where((lane8 == r0 + s) & (row8 >= s), a_s, a8)
                groups.append(a8)
            scores.append(jnp.concatenate(groups, axis=0).astype(BF16))

        for h in range(HEADS):
            cols = head_cols[h]
            st = st_s[h]
            v_bf = v_s[rows, cols]
            o = lax.dot_general(qin_s[rows, cols], st.astype(BF16), NT_DIMS, preferred_element_type=F32)
            o_s[rows, cols] = o + _dot(scores[h], v_bf)
            st_s[h] = st * dec_s[dec_row, cols] + lax.dot_general(
                v_bf, kst_s[rows, cols], TN_DIMS, preferred_element_type=F32)

    for ci in range(N_CHUNKS):
        chunk_step(ci)

    hg = hg_ref[...].astype(F32)
    parts = []
    for h in range(HEADS):
        cols = slice(h * DK, (h + 1) * DK)
        oh = o_s[:, cols]
        parts.append(oh * _rms_scale(oh) * hg[:, cols])
    on = jnp.concatenate(parts, axis=-1)
    ya = _dot((on * _sigmoid(r_s[:, R_OG:R_XP])).astype(BF16), w_ha_ref[...])

    xp_s[POOL_HIST:POOL_HIST + TM_MIX, :] = r_s[:, R_XP:R_GA]
    ys = []
    for gi, w in enumerate(POOL_WINDOWS):
        cols = slice(gi * POOL_GC, (gi + 1) * POOL_GC)
        cur = xp_s[POOL_HIST:POOL_HIST + TM_MIX, cols]
        acc = cur
        for j in range(1, w):
            acc = acc + xp_s[POOL_HIST - j:POOL_HIST - j + TM_MIX, cols]
        pooled = acc * (1.0 / w) - cur
        ys.append(_dot(pooled.astype(BF16), wgrp_ref[gi]))
    y = jnp.concatenate(ys, axis=-1) * pscale_ref[...].astype(F32)
    yb = _dot(y.astype(BF16), w_pb_ref[...])

    mixed = _sigmoid(r_s[:, R_GA:R_GB]) * ya + _sigmoid(r_s[:, R_GB:REST_COLS]) * yb
    h_ref[0] = x_ref[0].astype(F32) + _dot(mixed.astype(BF16), w_out_ref[...])


def _ffn_kernel(h_ref, g_ref, wg_ref, wu_ref, wd_ref, gf_ref, out_ref):
    h = h_ref[...].astype(F32)
    u = (h * _rms_scale(h) * g_ref[...].astype(F32)).astype(BF16)
    gate = _dot(u, wg_ref[...])
    up = _dot(u, wu_ref[...])
    act = (gate * _sigmoid(gate) * up).astype(BF16)
    h2 = h + _dot(act, wd_ref[...])
    out_ref[...] = (h2 * _rms_scale(h2) * gf_ref[...].astype(F32)).astype(out_ref.dtype)


def _const_spec(shape):
    nd = len(shape)
    return pl.BlockSpec(shape, lambda *_: (0,) * nd, pipeline_mode=pl.Buffered(1))


def kernel(x, meta_tokens, lb_logits, norm_mix_g, w_in, hg_norm_g, w_pool_grp, pool_scale, w_br_hgrn, w_br_pool, w_out, norm_ffn_g, w_ffn_gate, w_ffn_up, w_ffn_down, final_norm_g):
    bsz, seq, d = x.shape
    assert d == D_MODEL and seq % TM_MIX == 0 and (bsz * seq) % TM_FFN == 0
    assert w_in.shape[0] == 1, "single-layer block"
    assert w_in.shape[-1] == QFI_COLS + REST_COLS
    tiles_per_seq = seq // TM_MIX
    n_tiles = bsz * tiles_per_seq

    row = lambda a: a.reshape(1, -1)
    w_in_bf = w_in[0].astype(BF16)
    w_qfi = w_in_bf[:, :QFI_COLS].reshape(d, 3, N_GROUPS, GROUP_HEADS * DK).transpose(2, 0, 1, 3)
    w_qfi = w_qfi.reshape(N_GROUPS, d, GROUP_COLS)
    w_rest = w_in_bf[:, QFI_COLS:].reshape(d, N_GROUPS, REST_PART).transpose(1, 0, 2)
    w_all = jnp.concatenate([w_qfi, w_rest], axis=-1).transpose(1, 0, 2).reshape(d, N_GROUPS * GROUP_STRIDE)
    mixer_consts = (
        meta_tokens, lb_logits, row(norm_mix_g[0]), w_all,
        row(hg_norm_g[0]), w_pool_grp[0].astype(BF16), row(pool_scale[0]), w_br_hgrn[0].astype(BF16),
        w_br_pool[0].astype(BF16), w_out[0].astype(BF16))
    tile_spec = pl.BlockSpec((1, TM_MIX, d), lambda i: (i, 0, 0))
    h1 = pl.pallas_call(
        functools.partial(_mixer_kernel, tiles_per_seq=tiles_per_seq),
        grid=(n_tiles,),
        in_specs=[tile_spec] + [_const_spec(a.shape) for a in mixer_consts],
        out_specs=tile_spec,
        out_shape=jax.ShapeDtypeStruct((n_tiles, TM_MIX, d), F32),
        scratch_shapes=[
            pltpu.VMEM((TM_MIX, GROUP_COLS), F32),
            pltpu.VMEM((TM_MIX, GROUP_COLS), F32),
            pltpu.VMEM((TM_MIX, GROUP_COLS), F32),
            pltpu.VMEM((TM_MIX, GROUP_COLS), F32),
            pltpu.VMEM((TM_MIX, REST_COLS), F32),
            pltpu.VMEM((TM_MIX, HG_WIDTH), BF16),
            pltpu.VMEM((TM_MIX, HG_WIDTH), BF16),
            pltpu.VMEM((TM_MIX, HG_WIDTH), BF16),
            pltpu.VMEM((TM_MIX, HEADS * SEG_W), BF16),
            pltpu.VMEM((TM_MIX, HEADS * SEG_W), BF16),
            pltpu.VMEM((TM_MIX, HG_WIDTH), F32),
            pltpu.VMEM((TM_MIX, HG_WIDTH), F32),
            pltpu.VMEM((TM_MIX, HG_WIDTH), F32),
            pltpu.VMEM((N_CHUNKS * SUBLANES, HG_WIDTH), F32),
            pltpu.VMEM((TM_MIX, HG_WIDTH), F32),
            pltpu.VMEM((HEADS, DK, DK), F32),
            pltpu.VMEM((POOL_HIST + TM_MIX, POOL_WIDTH), F32),
        ],
        compiler_params=pltpu.CompilerParams(
            dimension_semantics=("arbitrary",),
            vmem_limit_bytes=VMEM_LIMIT_BYTES),
        name="mixer",
    )(x.reshape(n_tiles, TM_MIX, d), *mixer_consts)

    n_tok = bsz * seq
    ffn_consts = (row(norm_ffn_g[0]), w_ffn_gate[0].astype(BF16), w_ffn_up[0].astype(BF16),
                  w_ffn_down[0].astype(BF16), row(final_norm_g))
    out = pl.pallas_call(
        _ffn_kernel,
        grid=(n_tok // TM_FFN,),
        in_specs=[pl.BlockSpec((TM_FFN, d), lambda i: (i, 0))] + [_const_spec(a.shape) for a in ffn_consts],
        out_specs=pl.BlockSpec((TM_FFN, d), lambda i: (i, 0)),
        out_shape=jax.ShapeDtypeStruct((n_tok, d), x.dtype),
        compiler_params=pltpu.CompilerParams(
            dimension_semantics=("arbitrary",),
            vmem_limit_bytes=VMEM_LIMIT_BYTES),
        name="ffn",
    )(h1.reshape(n_tok, d), *ffn_consts)
    return out.reshape(bsz, seq, d)
```

```python
import functools

import jax
import jax.numpy as jnp
from jax import lax
from jax.experimental import pallas as pl
from jax.experimental.pallas import tpu as pltpu

D_MODEL = 1024
N_META = 16
CHUNK = 64
SUB = 16
N_SUB = CHUNK // SUB
HEADS = 8
DK = 128
LANES = 128
SUBLANES = 8
HG_WIDTH = HEADS * DK
POOL_WINDOWS = (2, 4, 8, 16)
POOL_GC = 128
POOL_WIDTH = POOL_GC * len(POOL_WINDOWS)
POOL_HIST = 16
EPS = 1e-6

TM_MIX = 256
N_CHUNKS = TM_MIX // CHUNK
TM_FFN = 512
VMEM_LIMIT_BYTES = 56 * 1024 * 1024

QFI_COLS = 3 * HG_WIDTH
GROUP_HEADS = 2
N_GROUPS = HEADS // GROUP_HEADS
GROUP_COLS = 3 * GROUP_HEADS * DK
G_Q, G_F, G_I = 0, GROUP_HEADS * DK, 2 * GROUP_HEADS * DK
REST_COLS = HG_WIDTH + POOL_WIDTH + 2 * D_MODEL
R_OG, R_XP = 0, HG_WIDTH
R_GA = R_XP + POOL_WIDTH
R_GB = R_GA + D_MODEL
REST_PART = REST_COLS // N_GROUPS
GROUP_STRIDE = GROUP_COLS + REST_PART
N_RECT_SEG = 3

F32 = jnp.float32
BF16 = jnp.bfloat16
NT_DIMS = (((1,), (1,)), ((), ()))
TN_DIMS = (((0,), (0,)), ((), ()))


def _sigmoid(x):
    return 1.0 / (1.0 + jnp.exp(-x))


def _rms_scale(x):
    return lax.rsqrt(jnp.mean(x * x, axis=-1, keepdims=True) + EPS)


def _dot(a, b):
    return jnp.dot(a, b, preferred_element_type=F32)


def _gate_terms(fp, lb):
    f = lb + (1.0 - lb) / (1.0 + jnp.exp(-fp))
    return 1.0 - f, jnp.log2(f)


def _block_cumsum(x, n_blocks):
    row = lax.broadcasted_iota(jnp.int32, (SUBLANES, DK), 0)
    out = []
    for i in range(n_blocks):
        halves = []
        for j in range(SUB // SUBLANES):
            v = x[i * SUB + j * SUBLANES:i * SUB + (j + 1) * SUBLANES]
            for sh in (1, 2, 4):
                v = v + jnp.where(row >= sh, pltpu.roll(v, sh, 0), 0.0)
            if halves:
                v = v + halves[-1][SUBLANES - 1:SUBLANES]
            halves.append(v)
        out.extend(halves)
    return jnp.concatenate(out, axis=0)


def _lower_bound(lb_ref):
    l = lb_ref[...].astype(F32)
    e = jnp.exp(l - jnp.max(l, axis=0, keepdims=True))
    return e[0:1] / jnp.sum(e, axis=0, keepdims=True)


def _mixer_kernel(x_ref, meta_ref, lb_ref, g_ref, w_all_ref, hg_ref, wgrp_ref,
                  pscale_ref, w_ha_ref, w_pb_ref, w_out_ref, h_ref,
                  p0_s, p1_s, p2_s, p3_s, r_s, qin_s, kst_s, v_s, sc_s, dec_s,
                  o_s, st_s, xp_s, *, tiles_per_seq):
    t_idx = pl.program_id(0) % tiles_per_seq
    lb_all = _lower_bound(lb_ref)
    g = g_ref[...].astype(F32)

    def normed(x):
        return (x * _rms_scale(x) * g).astype(BF16)

    @pl.when(t_idx == 0)
    def _():
        um = normed(meta_ref[...].astype(F32))
        pm = [_dot(um, w_all_ref[:, gi * GROUP_STRIDE:gi * GROUP_STRIDE + GROUP_COLS])
              for gi in range(N_GROUPS)]
        part, within = divmod(R_XP, REST_PART)
        assert within + POOL_WIDTH <= REST_PART
        xp0 = part * GROUP_STRIDE + GROUP_COLS + within
        xp_s[0:POOL_HIST, :] = _dot(um, w_all_ref[:, xp0:xp0 + POOL_WIDTH])
        for h in range(HEADS):
            gi, off = divmod(h, GROUP_HEADS)
            cols = slice(h * DK, (h + 1) * DK)
            k, lg2 = _gate_terms(pm[gi][:, G_F + off * DK:G_F + (off + 1) * DK], lb_all[:, cols])
            v = pm[gi][:, G_I + off * DK:G_I + (off + 1) * DK]
            c = _block_cumsum(lg2, N_META // SUB)
            kst = k * jnp.exp2(c[N_META - 1:N_META] - c)
            st_s[h] = lax.dot_general(v.astype(BF16), kst.astype(BF16), TN_DIMS,
                                      preferred_element_type=F32)

    @pl.when(t_idx > 0)
    def _():
        xp_s[0:POOL_HIST, :] = xp_s[TM_MIX:TM_MIX + POOL_HIST, :]

    x = x_ref[0].astype(F32)
    proj = _dot(normed(x), w_all_ref[...])
    p_s = (p0_s, p1_s, p2_s, p3_s)
    for gi in range(N_GROUPS):
        p_s[gi][...] = proj[:, gi * GROUP_STRIDE:gi * GROUP_STRIDE + GROUP_COLS]
        r_s[:, gi * REST_PART:(gi + 1) * REST_PART] = (
            proj[:, gi * GROUP_STRIDE + GROUP_COLS:(gi + 1) * GROUP_STRIDE])

    zero_blk = jnp.zeros((SUB, DK), F32)
    r64 = lax.broadcasted_iota(jnp.int32, (CHUNK, CHUNK), 0)
    c64 = lax.broadcasted_iota(jnp.int32, (CHUNK, CHUNK), 1)
    same_block = (r64 // SUB) == (c64 // SUB)
    row8 = lax.broadcasted_iota(jnp.int32, (SUBLANES, CHUNK), 0)
    lane8 = lax.broadcasted_iota(jnp.int32, (SUBLANES, CHUNK), 1)
    for gi in range(N_GROUPS):
        for ci, off in [(ci, off) for ci in range(N_CHUNKS) for off in range(GROUP_HEADS)]:
            h = gi * GROUP_HEADS + off
            rows = slice(ci * CHUNK, (ci + 1) * CHUNK)
            cols = slice(h * DK, (h + 1) * DK)
            qp = p_s[gi][rows, G_Q + off * DK:G_Q + (off + 1) * DK]
            fp = p_s[gi][rows, G_F + off * DK:G_F + (off + 1) * DK]
            v = p_s[gi][rows, G_I + off * DK:G_I + (off + 1) * DK]
            q = qp * _sigmoid(qp)
            k, lg2 = _gate_terms(fp, lb_all[:, cols])
            c = _block_cumsum(lg2, N_SUB)
            tot = [c[(i + 1) * SUB - 1:(i + 1) * SUB] for i in range(N_SUB)]
            tot_rows = jnp.concatenate([jnp.broadcast_to(t, (SUB, DK)) for t in tot], axis=0)
            q_e = q * jnp.exp2(c)
            k_r = k * jnp.exp2(tot_rows - c)
            qe = [q_e[i * SUB:(i + 1) * SUB] for i in range(N_SUB)]
            kr = [k_r[i * SUB:(i + 1) * SUB] for i in range(N_SUB)]

            pre = [None, tot[0], tot[0] + tot[1], tot[0] + tot[1] + tot[2]]
            q_in = jnp.concatenate([qe[0]] + [qe[i] * jnp.exp2(pre[i]) for i in range(1, N_SUB)], axis=0)
            post = [tot[1] + tot[2] + tot[3], tot[2] + tot[3], tot[3], None]
            k_st = jnp.concatenate([kr[i] * jnp.exp2(post[i]) for i in range(N_SUB - 1)] + [kr[3]], axis=0)
            z = zero_blk
            q_segs = [jnp.concatenate(b, axis=0) for b in (
                [z, qe[1], z, z], [z, z, z, qe[3]], [z, z, qe[2], qe[3] * jnp.exp2(tot[2])])]
            k_segs = [jnp.concatenate(b, axis=0) for b in (
                [kr[0], z, z, z], [z, z, kr[2], z], [kr[0] * jnp.exp2(tot[1]), kr[1], z, z])]
            z8 = jnp.zeros((SUBLANES, DK), F32)
            q_half, k_half = [], []
            for i in range(N_SUB):
                lo = slice(i * SUB, i * SUB + SUBLANES)
                hi = slice(i * SUB + SUBLANES, (i + 1) * SUB)
                c_mid = c[i * SUB + SUBLANES - 1:i * SUB + SUBLANES]
                q_half += [z8, q[hi] * jnp.exp2(c[hi] - c_mid)]
                k_half += [k[lo] * jnp.exp2(c_mid - c[lo]), z8]
            q_segs.append(jnp.concatenate(q_half, axis=0))
            k_segs.append(jnp.concatenate(k_half, axis=0))

            bf = lambda parts: jnp.concatenate([p.astype(BF16) for p in parts], axis=-1)
            a_off = lax.dot_general(bf(q_segs[:N_RECT_SEG]), bf(k_segs[:N_RECT_SEG]), NT_DIMS,
                                    preferred_element_type=F32)
            a_half = lax.dot_general(bf(q_segs[N_RECT_SEG:]), bf(k_segs[N_RECT_SEG:]), NT_DIMS,
                                     preferred_element_type=F32)
            a_off = a_off + jnp.where(same_block, a_half, 0.0)
            cs = c - jnp.log2(k)
            groups = []
            for r0 in range(0, CHUNK, SUBLANES):
                q8, c8, cs8 = (t[r0:r0 + SUBLANES] for t in (q, c, cs))
                a8 = a_off[r0:r0 + SUBLANES]
                for s in range(SUBLANES):
                    a_s = jnp.sum(q8 * jnp.exp2(c8 - cs8[s:s + 1]), axis=-1, keepdims=True)
                    a8 = jnp.where((lane8 == r0 + s) & (row8 >= s), a_s, a8)
                groups.append(a8)

            sc_s[h, rows, :] = jnp.concatenate(groups, axis=0).astype(BF16)
            qin_s[rows, cols] = q_in.astype(BF16)
            kst_s[rows, cols] = k_st.astype(BF16)
            v_s[rows, cols] = v.astype(BF16)
            dec_s[ci * SUBLANES:ci * SUBLANES + 1, cols] = jnp.exp2(pre[3] + tot[3])

    for ci in range(N_CHUNKS):
        rows = slice(ci * CHUNK, (ci + 1) * CHUNK)
        for h in range(HEADS):
            cols = slice(h * DK, (h + 1) * DK)
            st = st_s[h]
            v_bf = v_s[rows, cols]
            o = lax.dot_general(qin_s[rows, cols], st.astype(BF16), NT_DIMS, preferred_element_type=F32)
            o_s[rows, cols] = o + _dot(sc_s[h, rows, :], v_bf)
            st_s[h] = st * dec_s[ci * SUBLANES:ci * SUBLANES + 1, cols] + lax.dot_general(
                v_bf, kst_s[rows, cols], TN_DIMS, preferred_element_type=F32)

    hg = hg_ref[...].astype(F32)
    parts = []
    for h in range(HEADS):
        cols = slice(h * DK, (h + 1) * DK)
        oh = o_s[:, cols]
        parts.append(oh * _rms_scale(oh) * hg[:, cols])
    on = jnp.concatenate(parts, axis=-1)
    ya = _dot((on * _sigmoid(r_s[:, R_OG:R_XP])).astype(BF16), w_ha_ref[...])

    xp_s[POOL_HIST:POOL_HIST + TM_MIX, :] = r_s[:, R_XP:R_GA]
    ys = []
    for gi, w in enumerate(POOL_WINDOWS):
        cols = slice(gi * POOL_GC, (gi + 1) * POOL_GC)
        cur = xp_s[POOL_HIST:POOL_HIST + TM_MIX, cols]
        acc = cur
        for j in range(1, w):
            acc = acc + xp_s[POOL_HIST - j:POOL_HIST - j + TM_MIX, cols]
        pooled = acc * (1.0 / w) - cur
        ys.append(_dot(pooled.astype(BF16), wgrp_ref[gi]))
    y = jnp.concatenate(ys, axis=-1) * pscale_ref[...].astype(F32)
    yb = _dot(y.astype(BF16), w_pb_ref[...])

    mixed = _sigmoid(r_s[:, R_GA:R_GB]) * ya + _sigmoid(r_s[:, R_GB:REST_COLS]) * yb
    h_ref[0] = x_ref[0].astype(F32) + _dot(mixed.astype(BF16), w_out_ref[...])


def _ffn_kernel(h_ref, g_ref, wg_ref, wu_ref, wd_ref, gf_ref, out_ref):
    h = h_ref[...].astype(F32)
    u = (h * _rms_scale(h) * g_ref[...].astype(F32)).astype(BF16)
    gate = _dot(u, wg_ref[...])
    up = _dot(u, wu_ref[...])
    act = (gate * _sigmoid(gate) * up).astype(BF16)
    h2 = h + _dot(act, wd_ref[...])
    out_ref[...] = (h2 * _rms_scale(h2) * gf_ref[...].astype(F32)).astype(out_ref.dtype)


def _const_spec(shape):
    nd = len(shape)
    return pl.BlockSpec(shape, lambda *_: (0,) * nd, pipeline_mode=pl.Buffered(1))


def kernel(x, meta_tokens, lb_logits, norm_mix_g, w_in, hg_norm_g, w_pool_grp, pool_scale, w_br_hgrn, w_br_pool, w_out, norm_ffn_g, w_ffn_gate, w_ffn_up, w_ffn_down, final_norm_g):
    bsz, seq, d = x.shape
    assert d == D_MODEL and seq % TM_MIX == 0 and (bsz * seq) % TM_FFN == 0
    assert w_in.shape[0] == 1, "single-layer block"
    assert w_in.shape[-1] == QFI_COLS + REST_COLS
    tiles_per_seq = seq // TM_MIX
    n_tiles = bsz * tiles_per_seq

    row = lambda a: a.reshape(1, -1)
    gw = GROUP_HEADS * DK
    pieces = []
    for gi in range(N_GROUPS):
        pieces += [w_in[0, :, part * HG_WIDTH + gi * gw:part * HG_WIDTH + (gi + 1) * gw] for part in range(3)]
        pieces.append(w_in[0, :, QFI_COLS + gi * REST_PART:QFI_COLS + (gi + 1) * REST_PART])
    w_all = jnp.concatenate(pieces, axis=-1).astype(BF16)
    mixer_consts = (
        meta_tokens, lb_logits, row(norm_mix_g[0]), w_all,
        row(hg_norm_g[0]), w_pool_grp[0].astype(BF16), row(pool_scale[0]), w_br_hgrn[0].astype(BF16),
        w_br_pool[0].astype(BF16), w_out[0].astype(BF16))
    tile_spec = pl.BlockSpec((1, TM_MIX, d), lambda i: (i, 0, 0))
    h1 = pl.pallas_call(
        functools.partial(_mixer_kernel, tiles_per_seq=tiles_per_seq),
        grid=(n_tiles,),
        in_specs=[tile_spec] + [_const_spec(a.shape) for a in mixer_consts],
        out_specs=tile_spec,
        out_shape=jax.ShapeDtypeStruct((n_tiles, TM_MIX, d), F32),
        scratch_shapes=[
            pltpu.VMEM((TM_MIX, GROUP_COLS), F32),
            pltpu.VMEM((TM_MIX, GROUP_COLS), F32),
            pltpu.VMEM((TM_MIX, GROUP_COLS), F32),
            pltpu.VMEM((TM_MIX, GROUP_COLS), F32),
            pltpu.VMEM((TM_MIX, REST_COLS), F32),
            pltpu.VMEM((TM_MIX, HG_WIDTH), BF16),
            pltpu.VMEM((TM_MIX, HG_WIDTH), BF16),
            pltpu.VMEM((TM_MIX, HG_WIDTH), BF16),
            pltpu.VMEM((HEADS, TM_MIX, CHUNK), BF16),
            pltpu.VMEM((N_CHUNKS * SUBLANES, HG_WIDTH), F32),
            pltpu.VMEM((TM_MIX, HG_WIDTH), F32),
            pltpu.VMEM((HEADS, DK, DK), F32),
            pltpu.VMEM((POOL_HIST + TM_MIX, POOL_WIDTH), F32),
        ],
        compiler_params=pltpu.CompilerParams(
            dimension_semantics=("arbitrary",),
            vmem_limit_bytes=VMEM_LIMIT_BYTES),
        name="mixer",
    )(x.reshape(n_tiles, TM_MIX, d), *mixer_consts)

    n_tok = bsz * seq
    ffn_consts = (row(norm_ffn_g[0]), w_ffn_gate[0].astype(BF16), w_ffn_up[0].astype(BF16),
                  w_ffn_down[0].astype(BF16), row(final_norm_g))
    out = pl.pallas_call(
        _ffn_kernel,
        grid=(n_tok // TM_FFN,),
        in_specs=[pl.BlockSpec((TM_FFN, d), lambda i: (i, 0))] + [_const_spec(a.shape) for a in ffn_consts],
        out_specs=pl.BlockSpec((TM_FFN, d), lambda i: (i, 0)),
        out_shape=jax.ShapeDtypeStruct((n_tok, d), x.dtype),
        compiler_params=pltpu.CompilerParams(
            dimension_semantics=("arbitrary",),
            vmem_limit_bytes=VMEM_LIMIT_BYTES),
        name="ffn",
    )(h1.reshape(n_tok, d), *ffn_consts)
    return out.reshape(bsz, seq, d)
```
